```python
import math
import jax, jax.numpy as jnp
from jax import lax
import numpy as np

D_MODEL = 1024
BATCH = 4
SEQ = 8192
DEPTH = 2

HEAD_DIM = 64
Q_HEADS = 8
KV_HEADS = 2
GROUP = Q_HEADS // KV_HEADS
ATTN_WIDTH = Q_HEADS * HEAD_DIM
KV_WIDTH = KV_HEADS * HEAD_DIM
WINDOW = 128
BLOCK = 128
POOL_WIDTH = D_MODEL - ATTN_WIDTH
POOL_WINDOWS = (2, 4, 8, 16)
POOL_GROUPS = len(POOL_WINDOWS)
POOL_GC = POOL_WIDTH // POOL_GROUPS
EVEN_IN = ATTN_WIDTH + 2 * KV_WIDTH + ATTN_WIDTH + POOL_WIDTH + POOL_WIDTH
EVEN_MIX = ATTN_WIDTH + POOL_WIDTH
CONV_WIDTH = D_MODEL
CONV_K = 31
ODD_IN = 3 * CONV_WIDTH
EPS = 1e-6
NEG = -1e30
N_EVEN = (DEPTH + 1) // 2
N_ODD = DEPTH // 2

kernel_name = "hybrid_swa_pool_conformer_sandwich"


def rms_norm(x, g):
    xf = x.astype(jnp.float32)
    y = xf * lax.rsqrt(jnp.mean(xf * xf, axis=-1, keepdims=True) + EPS)
    return (y * g.astype(jnp.float32)).astype(x.dtype)


def alibi_slopes(n):
    return jnp.exp2(-8.0 * jnp.arange(1, n + 1, dtype=jnp.float32) / n)


def sliding_window_attention(q, k, v, sinks):
    B, S, _ = q.shape
    nb = S // BLOCK
    q = q.reshape(B, nb, BLOCK, KV_HEADS, GROUP, HEAD_DIM)
    k = k.reshape(B, nb, BLOCK, KV_HEADS, HEAD_DIM)
    v = v.reshape(B, nb, BLOCK, KV_HEADS, HEAD_DIM)
    kpad = jnp.zeros_like(k[:, :1])
    vpad = jnp.zeros_like(v[:, :1])
    kk = jnp.concatenate([jnp.concatenate([kpad, k[:, :-1]], axis=1), k], axis=2)
    vv = jnp.concatenate([jnp.concatenate([vpad, v[:, :-1]], axis=1), v], axis=2)
    scores = jnp.einsum('bnqkgd,bnskd->bnkgqs', q, kk).astype(jnp.float32) * (HEAD_DIM ** -0.5)
    qi = jnp.arange(BLOCK)[:, None] + BLOCK
    sj = jnp.arange(2 * BLOCK)[None, :]
    dist = qi - sj
    key_pos = jnp.arange(nb)[:, None, None] * BLOCK + sj[None] - BLOCK
    valid = (dist >= 0)[None] & (dist < WINDOW)[None] & (key_pos >= 0)
    slopes = alibi_slopes(Q_HEADS).reshape(KV_HEADS, GROUP)
    bias = -slopes[:, :, None, None] * dist.astype(jnp.float32)
    scores = jnp.where(valid[None, :, None, None], scores + bias, NEG)
    sink = sinks.astype(jnp.float32).reshape(KV_HEADS, GROUP)[None, None, :, :, None, None]
    mx = jnp.maximum(jnp.max(scores, axis=-1, keepdims=True), sink)
    p = jnp.exp(scores - mx)
    p = p / (jnp.sum(p, axis=-1, keepdims=True) + jnp.exp(sink - mx))
    out = jnp.einsum('bnkgqs,bnskd->bnqkgd', p.astype(vv.dtype), vv)
    return out.reshape(B, S, ATTN_WIDTH)


def multiscale_pool(u, pool_w, pool_scale):
    B, S, _ = u.shape
    uf = u.astype(jnp.float32).reshape(B, S, POOL_GROUPS, POOL_GC)
    cs = jnp.concatenate([jnp.zeros_like(uf[:, :1]), jnp.cumsum(uf, axis=1)], axis=1)
    t = jnp.arange(S)[:, None]
    win = jnp.array(POOL_WINDOWS, dtype=jnp.int32)[None, :]
    lo = jnp.maximum(t + 1 - win, 0)
    cnt = (t + 1 - lo).astype(jnp.float32)
    lower = cs[:, lo, jnp.arange(POOL_GROUPS)[None, :]]
    pooled = (cs[:, 1:] - lower) / cnt[None, :, :, None] - uf
    y = jnp.einsum('bsgc,gcd->bsgd', pooled, pool_w.astype(jnp.float32))
    y = y * pool_scale.astype(jnp.float32).reshape(POOL_GROUPS, POOL_GC)
    return y.reshape(B, S, POOL_WIDTH).astype(u.dtype)


def even_mixer(h, w_in, sinks, pool_w, pool_scale, w_out):
    proj = h @ w_in
    splits = np.cumsum([ATTN_WIDTH, KV_WIDTH, KV_WIDTH, ATTN_WIDTH, POOL_WIDTH]).tolist()
    q, k, v, ga, u, gb = jnp.split(proj, splits, axis=-1)
    ya = sliding_window_attention(q, k, v, sinks).astype(h.dtype) * jax.nn.silu(ga)
    yb = multiscale_pool(u, pool_w, pool_scale) * jax.nn.silu(gb)
    return jnp.concatenate([ya, yb], axis=-1) @ w_out


def odd_mixer(h, w_in, dw_w, dw_b, ln_g, ln_b, w_out):
    proj = h @ w_in
    a, b, gate = jnp.split(proj, [CONV_WIDTH, 2 * CONV_WIDTH], axis=-1)
    glu = a * jax.nn.sigmoid(b)
    conv = lax.conv_general_dilated(
        glu, dw_w.astype(glu.dtype), window_strides=(1,), padding=[(CONV_K - 1, 0)],
        dimension_numbers=('NWC', 'WIO', 'NWC'), feature_group_count=CONV_WIDTH)
    cf = conv.astype(jnp.float32) + dw_b.astype(jnp.float32)
    mu = jnp.mean(cf, axis=-1, keepdims=True)
    var = jnp.mean(jnp.square(cf - mu), axis=-1, keepdims=True)
    cn = (cf - mu) * lax.rsqrt(var + EPS) * ln_g.astype(jnp.float32) + ln_b.astype(jnp.float32)
    y = jax.nn.silu(cn).astype(h.dtype) * jax.nn.silu(gate)
    return y @ w_out


def setup_inputs(seed: int = 0) -> dict:
    key = jax.random.key(seed)
    ks = jax.random.split(key, 16)
    f32 = jnp.float32
    nrm = lambda k, shape, s: jax.random.normal(k, shape, f32) * s
    return {
        'x': nrm(ks[0], (BATCH, SEQ, D_MODEL), 1.0),
        'pre_norm': 1.0 + nrm(ks[1], (DEPTH, D_MODEL), 0.05),
        'post_norm': 1.0 + nrm(ks[2], (DEPTH, D_MODEL), 0.05),
        'a_w_in': nrm(ks[3], (N_EVEN, D_MODEL, EVEN_IN), D_MODEL ** -0.5),
        'a_sinks': nrm(ks[4], (N_EVEN, Q_HEADS), 0.5),
        'b_pool_w': nrm(ks[5], (N_EVEN, POOL_GROUPS, POOL_GC, POOL_GC), POOL_GC ** -0.5),
        'b_pool_scale': 1.0 + nrm(ks[6], (N_EVEN, POOL_WIDTH), 0.1),
        'ab_w_out': nrm(ks[7], (N_EVEN, EVEN_MIX, D_MODEL), EVEN_MIX ** -0.5),
        'c_w_in': nrm(ks[8], (N_ODD, D_MODEL, ODD_IN), D_MODEL ** -0.5),
        'c_dw_w': nrm(ks[9], (N_ODD, CONV_K, 1, CONV_WIDTH), CONV_K ** -0.5),
        'c_dw_b': nrm(ks[10], (N_ODD, CONV_WIDTH), 0.02),
        'c_ln_g': 1.0 + nrm(ks[11], (N_ODD, CONV_WIDTH), 0.05),
        'c_ln_b': nrm(ks[12], (N_ODD, CONV_WIDTH), 0.02),
        'c_w_out': nrm(ks[13], (N_ODD, CONV_WIDTH, D_MODEL), CONV_WIDTH ** -0.5),
    }


def reference(x, pre_norm, post_norm, a_w_in, a_sinks, b_pool_w, b_pool_scale, ab_w_out,
              c_w_in, c_dw_w, c_dw_b, c_ln_g, c_ln_b, c_w_out):
    for layer in range(DEPTH):
        h = rms_norm(x, pre_norm[layer])
        if layer % 2 == 0:
            i = layer // 2
            y = even_mixer(h, a_w_in[i], a_sinks[i], b_pool_w[i], b_pool_scale[i], ab_w_out[i])
        else:
            i = layer // 2
            y = odd_mixer(h, c_w_in[i], c_dw_w[i], c_dw_b[i], c_ln_g[i], c_ln_b[i], c_w_out[i])
        x = x + rms_norm(y, post_norm[layer])
    return x
```

```python
import functools

import jax
import jax.numpy as jnp
from jax import lax
from jax.experimental import pallas as pl
from jax.experimental.pallas import tpu as pltpu

D_MODEL = 1024
HEAD_DIM = 64
Q_HEADS = 8
KV_HEADS = 2
ATTN_WIDTH = Q_HEADS * HEAD_DIM
KV_WIDTH = KV_HEADS * HEAD_DIM
BLOCK = 128
POOL_WIDTH = D_MODEL - ATTN_WIDTH
POOL_WINDOWS = (2, 4, 8, 16)
POOL_GC = POOL_WIDTH // len(POOL_WINDOWS)
EVEN_IN = ATTN_WIDTH + 2 * KV_WIDTH + ATTN_WIDTH + POOL_WIDTH + POOL_WIDTH
CONV_K = 31
ODD_IN = 3 * D_MODEL
EPS = 1e-6
NEG = -1e30

LANES = 128
SUBLANES = 8
SEQ_TILE = 512
POOL_HIST = 24
CONV_HIST = 32
CONV_ROWS = 64
VMEM_LIMIT_BYTES = 56 * 1024 * 1024

_F32 = jnp.float32
_BF16 = jnp.bfloat16


def _rms(xf, g):
    return xf * lax.rsqrt(jnp.mean(xf * xf, axis=-1, keepdims=True) + EPS) * g


def _silu(v):
    return v * jax.nn.sigmoid(v)


def _build_attn_tables(bias_ref, mask_ref):
    shape = (2 * BLOCK, 4 * BLOCK)
    r = lax.broadcasted_iota(jnp.int32, shape, 0)
    col = lax.broadcasted_iota(jnp.int32, shape, 1)
    c = r >> 7
    i = r & (BLOCK - 1)
    e = col >> 8
    sj = col & (2 * BLOCK - 1)
    dist = i + BLOCK - sj
    valid0 = (dist >= 0) & (dist < BLOCK)
    distf = dist.astype(_F32)
    for var in range(2):
        valid = valid0 & (sj >= BLOCK) if var == 1 else valid0
        mask_ref[var] = jnp.where(valid, 1.0, 0.0).astype(_F32)
        for j in range(KV_HEADS):
            head = 4 * j + 2 * c + e
            slope = jnp.exp2(-(head + 1).astype(_F32))
            bias_ref[var, j] = jnp.where(valid, -slope * distf, NEG)


def _even_kernel(x_ref, gpre_ref, gpost_ref, win_ref, sink_ref, poolw_ref, pscale_ref, wout_ref,
                 o_ref, kst_ref, vst_ref, q_ref, att_ref, u_ref, t1_ref, t2_ref, bias_ref, mask_ref,
                 *, ts):
    s = pl.program_id(1)
    nblk = ts // BLOCK

    @pl.when(s == 0)
    def _start_of_sequence():
        _build_attn_tables(bias_ref, mask_ref)
        kst_ref[:, :, 0:BLOCK, :] = jnp.zeros((KV_HEADS, 2, BLOCK, LANES), _BF16)
        vst_ref[:, :, 0:BLOCK, :] = jnp.zeros((KV_HEADS, 2, BLOCK, LANES), _BF16)
        u_ref[0:POOL_HIST, :] = jnp.zeros((POOL_HIST, POOL_WIDTH), _F32)
        t1_ref[0:SUBLANES, :] = jnp.zeros((SUBLANES, POOL_GC), _F32)
        t2_ref[0:SUBLANES, :] = jnp.zeros((SUBLANES, POOL_GC), _F32)

    @pl.when(s > 0)
    def _carry_history():
        kst_ref[:, :, 0:BLOCK, :] = kst_ref[:, :, ts:ts + BLOCK, :]
        vst_ref[:, :, 0:BLOCK, :] = vst_ref[:, :, ts:ts + BLOCK, :]
        u_ref[SUBLANES:POOL_HIST, :] = u_ref[ts + SUBLANES:ts + POOL_HIST, :]

    x = x_ref[0]
    h = _rms(x, gpre_ref[...]).astype(_BF16)
    proj = jnp.dot(h, win_ref[...], preferred_element_type=_F32)

    o_q, o_k, o_v = 0, ATTN_WIDTH, ATTN_WIDTH + KV_WIDTH
    o_ga = o_v + KV_WIDTH
    o_u = o_ga + ATTN_WIDTH
    o_gb = o_u + POOL_WIDTH

    q_ref[...] = (proj[:, o_q:o_q + ATTN_WIDTH] * (HEAD_DIM ** -0.5)).astype(_BF16)

    lane = lax.broadcasted_iota(jnp.int32, (ts, LANES), 1)
    lo = lane < HEAD_DIM
    for src_off, dst in ((o_k, kst_ref), (o_v, vst_ref)):
        t2 = proj[:, src_off:src_off + KV_WIDTH]
        t2r = pltpu.roll(t2, HEAD_DIM, 1)
        dst[0, 0, BLOCK:, :] = jnp.where(lo, t2, 0.0).astype(_BF16)
        dst[0, 1, BLOCK:, :] = jnp.where(lo, 0.0, t2r).astype(_BF16)
        dst[1, 0, BLOCK:, :] = jnp.where(lo, t2r, 0.0).astype(_BF16)
        dst[1, 1, BLOCK:, :] = jnp.where(lo, 0.0, t2).astype(_BF16)

    lane_b = lax.broadcasted_iota(jnp.int32, (2 * BLOCK, LANES), 1)
    ones_even = jnp.where(lane_b < HEAD_DIM, 1.0, 0.0).astype(_BF16)
    ones_odd = jnp.where(lane_b < HEAD_DIM, 0.0, 1.0).astype(_BF16)
    lo_blk = lax.broadcasted_iota(jnp.int32, (BLOCK, LANES), 1) < HEAD_DIM

    def attn_block(n):
        r0 = n * BLOCK
        var = jnp.where(s == 0, 1, 0) if n == 0 else 0
        msk = mask_ref[var]
        for j in range(KV_HEADS):
            qcat = jnp.concatenate(
                [q_ref[pl.ds(r0, BLOCK), (2 * j) * LANES:(2 * j + 1) * LANES],
                 q_ref[pl.ds(r0, BLOCK), (2 * j + 1) * LANES:(2 * j + 2) * LANES]], axis=0)
            kcat = jnp.concatenate(
                [kst_ref[j, 0, pl.ds(r0, 2 * BLOCK), :], kst_ref[j, 1, pl.ds(r0, 2 * BLOCK), :]], axis=0)
            sc = lax.dot_general(qcat, kcat, (((1,), (1,)), ((), ())),
                                 preferred_element_type=_F32)
            sc = sc * msk + bias_ref[var, j]
            p_rows, ex_rows = [], []
            for c in range(2):
                sc_c = sc[c * BLOCK:(c + 1) * BLOCK]
                halves, exs = [], []
                for e in range(2):
                    sink = sink_ref[4 * j + 2 * c + e]
                    sc_h = sc_c[:, e * 2 * BLOCK:(e + 1) * 2 * BLOCK]
                    m = jnp.maximum(jnp.max(sc_h, axis=-1, keepdims=True), sink)
                    halves.append(jnp.exp(sc_h - m))
                    exs.append(jnp.exp(sink - m))
                p_rows.append(jnp.concatenate(halves, axis=1))
                ex_rows.append(jnp.where(lo_blk, exs[0], exs[1]))
            p = jnp.concatenate(p_rows, axis=0).astype(_BF16)
            ex = jnp.concatenate(ex_rows, axis=0)
            vcat = jnp.concatenate(
                [jnp.concatenate([vst_ref[j, 0, pl.ds(r0, 2 * BLOCK), :], ones_even], axis=1),
                 jnp.concatenate([vst_ref[j, 1, pl.ds(r0, 2 * BLOCK), :], ones_odd], axis=1)], axis=0)
            res = jnp.dot(p, vcat, preferred_element_type=_F32)
            o = res[:, 0:LANES] / (ex + res[:, LANES:2 * LANES])
            att_ref[pl.ds(r0, BLOCK), (2 * j) * LANES:(2 * j + 1) * LANES] = o[0:BLOCK]
            att_ref[pl.ds(r0, BLOCK), (2 * j + 1) * LANES:(2 * j + 2) * LANES] = o[BLOCK:2 * BLOCK]

    for n in range(nblk):
        attn_block(n)

    ya = att_ref[...] * _silu(proj[:, o_ga:o_ga + ATTN_WIDTH])

    u_ref[POOL_HIST:, :] = proj[:, o_u:o_u + POOL_WIDTH]
    ext = ts + POOL_HIST - SUBLANES
    t_abs = s * ts + lax.broadcasted_iota(jnp.int32, (ts, POOL_GC), 0)
    yb_parts = []
    for g, w in enumerate(POOL_WINDOWS):
        cols = slice(g * POOL_GC, (g + 1) * POOL_GC)
        src = None
        span = 1
        bufs = (t1_ref, t2_ref)
        k = 0
        while span < w:
            if src is None:
                cur = u_ref[pl.ds(SUBLANES, ext), cols] + u_ref[pl.ds(SUBLANES - span, ext), cols]
            else:
                cur = src[pl.ds(SUBLANES, ext), :] + src[pl.ds(SUBLANES - span, ext), :]
            span *= 2
            if span < w:
                bufs[k % 2][pl.ds(SUBLANES, ext), :] = cur
                src = bufs[k % 2]
                k += 1
        win_sum = cur[POOL_HIST - SUBLANES:, :]
        cnt = jnp.minimum(t_abs + 1, w).astype(_F32)
        u_g = u_ref[pl.ds(POOL_HIST, ts), cols]
        pooled = (win_sum / cnt - u_g).astype(_BF16)
        y_g = jnp.dot(pooled, poolw_ref[g], preferred_element_type=_F32)
        yb_parts.append(y_g * pscale_ref[:, cols])
    yb = jnp.concatenate(yb_parts, axis=1) * _silu(proj[:, o_gb:o_gb + POOL_WIDTH])

    y = jnp.concatenate([ya, yb], axis=1).astype(_BF16)
    mix = jnp.dot(y, wout_ref[...], preferred_element_type=_F32)
    o_ref[0] = x + _rms(mix, gpost_ref[...])


def _odd_kernel(x_ref, gpre_ref, gpost_ref, win_ref, dww_ref, dwb_ref, lng_ref, lnb_ref, wout_ref,
                o_ref, g_ref, cf_ref, *, ts):
    s = pl.program_id(1)

    @pl.when(s == 0)
    def _start_of_sequence():
        g_ref[0:CONV_HIST, :] = jnp.zeros((CONV_HIST, D_MODEL), _F32)

    @pl.when(s > 0)
    def _carry_history():
        g_ref[0:CONV_HIST, :] = g_ref[ts:ts + CONV_HIST, :]

    x = x_ref[0]
    h = _rms(x, gpre_ref[...]).astype(_BF16)
    proj = jnp.dot(h, win_ref[...], preferred_element_type=_F32)
    g_ref[CONV_HIST:, :] = proj[:, 0:D_MODEL] * jax.nn.sigmoid(proj[:, D_MODEL:2 * D_MODEL])

    first = CONV_HIST - (CONV_K - 1)
    for lc in range(D_MODEL // LANES):
        cols = slice(lc * LANES, (lc + 1) * LANES)
        w_l = dww_ref[:, cols]
        b_l = dwb_ref[:, cols]

        for rc in range(ts // CONV_ROWS):
            base = rc * CONV_ROWS
            acc = jnp.zeros((CONV_ROWS, LANES), _F32)
            for j in range(CONV_K):
                acc = acc + w_l[j:j + 1, :] * g_ref[base + first + j:base + first + j + CONV_ROWS, cols]
            cf_ref[base:base + CONV_ROWS, cols] = acc + b_l

    cf = cf_ref[...]
    mu = jnp.mean(cf, axis=-1, keepdims=True)
    d = cf - mu
    var = jnp.mean(d * d, axis=-1, keepdims=True)
    cn = d * lax.rsqrt(var + EPS) * lng_ref[...] + lnb_ref[...]
    y = (_silu(cn) * _silu(proj[:, 2 * D_MODEL:3 * D_MODEL])).astype(_BF16)
    mix = jnp.dot(y, wout_ref[...], preferred_element_type=_F32)
    o_ref[0] = x + _rms(mix, gpost_ref[...])


def _full(shape):
    return pl.BlockSpec(shape, lambda b, s: (0,) * len(shape))


def _compiler_params():
    return pltpu.CompilerParams(dimension_semantics=("arbitrary", "arbitrary"),
                                vmem_limit_bytes=VMEM_LIMIT_BYTES)


def _even_layer(x, gpre, gpost, w_in, sinks, pool_w, pool_scale, w_out, ts):
    b, seq, d = x.shape
    tile = pl.BlockSpec((1, ts, d), lambda i, j: (i, j, 0))
    return pl.pallas_call(
        functools.partial(_even_kernel, ts=ts),
        out_shape=jax.ShapeDtypeStruct(x.shape, x.dtype),
        grid=(b, seq // ts),
        in_specs=[tile, _full((1, d)), _full((1, d)), _full((d, EVEN_IN)),
                  pl.BlockSpec(memory_space=pltpu.SMEM),
                  _full((len(POOL_WINDOWS), POOL_GC, POOL_GC)), _full((1, POOL_WIDTH)),
                  _full((d, d))],
        out_specs=tile,
        scratch_shapes=[
            pltpu.VMEM((KV_HEADS, 2, ts + BLOCK, LANES), _BF16),
            pltpu.VMEM((KV_HEADS, 2, ts + BLOCK, LANES), _BF16),
            pltpu.VMEM((ts, ATTN_WIDTH), _BF16),
            pltpu.VMEM((ts, ATTN_WIDTH), _F32),
            pltpu.VMEM((ts + POOL_HIST, POOL_WIDTH), _F32),
            pltpu.VMEM((ts + POOL_HIST, POOL_GC), _F32),
            pltpu.VMEM((ts + POOL_HIST, POOL_GC), _F32),
            pltpu.VMEM((2, KV_HEADS, 2 * BLOCK, 4 * BLOCK), _F32),
            pltpu.VMEM((2, 2 * BLOCK, 4 * BLOCK), _F32),
        ],
        compiler_params=_compiler_params(),
        name="even_layer",
    )(x, gpre, gpost, w_in, sinks, pool_w, pool_scale, w_out)


def _odd_layer(x, gpre, gpost, w_in, dw_w, dw_b, ln_g, ln_b, w_out, ts):
    b, seq, d = x.shape
    tile = pl.BlockSpec((1, ts, d), lambda i, j: (i, j, 0))
    return pl.pallas_call(
        functools.partial(_odd_kernel, ts=ts),
        out_shape=jax.ShapeDtypeStruct(x.shape, x.dtype),
        grid=(b, seq // ts),
        in_specs=[tile, _full((1, d)), _full((1, d)), _full((d, ODD_IN)), _full((CONV_K, d)),
                  _full((1, d)), _full((1, d)), _full((1, d)), _full((d, d))],
        out_specs=tile,
        scratch_shapes=[
            pltpu.VMEM((ts + CONV_HIST, d), _F32),
            pltpu.VMEM((ts, d), _F32),
        ],
        compiler_params=_compiler_params(),
        name="odd_layer",
    )(x, gpre, gpost, w_in, dw_w, dw_b, ln_g, ln_b, w_out)


def kernel(x, pre_norm, post_norm, a_w_in, a_sinks, b_pool_w, b_pool_scale, ab_w_out,
           c_w_in, c_dw_w, c_dw_b, c_ln_g, c_ln_b, c_w_out):
    depth = pre_norm.shape[0]
    ts = SEQ_TILE
    assert x.shape[1] % ts == 0 and ts % BLOCK == 0 and ts % CONV_ROWS == 0
    row = lambda v: v.reshape(1, -1).astype(_F32)
    for layer in range(depth):
        i = layer // 2
        if layer % 2 == 0:
            x = _even_layer(x, row(pre_norm[layer]), row(post_norm[layer]),
                            a_w_in[i].astype(_BF16), a_sinks[i].astype(_F32),
                            b_pool_w[i].astype(_BF16), row(b_pool_scale[i]),
                            ab_w_out[i].astype(_BF16), ts)
        else:
            x = _odd_layer(x, row(pre_norm[layer]), row(post_norm[layer]),
                           c_w_in[i].astype(_BF16), c_dw_w[i].reshape(CONV_K, -1).astype(_F32),
                           row(c_dw_b[i]), row(c_ln_g[i]), row(c_ln_b[i]),
                           c_w_out[i].astype(_BF16), ts)
    return x
```

```python
import functools

import jax
import jax.numpy as jnp
from jax import lax
from jax.experimental import pallas as pl
from jax.experimental.pallas import tpu as pltpu

D_MODEL = 1024
HEAD_DIM = 64
Q_HEADS = 8
KV_HEADS = 2
ATTN_WIDTH = Q_HEADS * HEAD_DIM
KV_WIDTH = KV_HEADS * HEAD_DIM
BLOCK = 128
POOL_WIDTH = D_MODEL - ATTN_WIDTH
POOL_WINDOWS = (2, 4, 8, 16)
POOL_GC = POOL_WIDTH // len(POOL_WINDOWS)
EVEN_IN = ATTN_WIDTH + 2 * KV_WIDTH + ATTN_WIDTH + POOL_WIDTH + POOL_WIDTH
CONV_K = 31
ODD_IN = 3 * D_MODEL
EPS = 1e-6
NEG = -1e30

LANES = 128
SUBLANES = 8
N_SLAB = D_MODEL // LANES
SEQ_TILE = 512
POOL_HIST = 24
CONV_HIST = 32
CONV_ROWS = 64
VMEM_LIMIT_BYTES = 56 * 1024 * 1024

_F32 = jnp.float32
_BF16 = jnp.bfloat16


def _rms(xf, g):
    return xf * lax.rsqrt(jnp.mean(xf * xf, axis=-1, keepdims=True) + EPS) * g


def _silu(v):
    return v * jax.nn.sigmoid(v)


def _build_attn_tables(bias_ref, mask_ref):
    shape = (2 * BLOCK, 4 * BLOCK)
    r = lax.broadcasted_iota(jnp.int32, shape, 0)
    col = lax.broadcasted_iota(jnp.int32, shape, 1)
    c = r >> 7
    i = r & (BLOCK - 1)
    e = col >> 8
    sj = col & (2 * BLOCK - 1)
    dist = i + BLOCK - sj
    valid0 = (dist >= 0) & (dist < BLOCK)
    distf = dist.astype(_F32)
    for var in range(2):
        valid = valid0 & (sj >= BLOCK) if var == 1 else valid0
        mask_ref[var] = jnp.where(valid, 1.0, 0.0).astype(_F32)
        for j in range(KV_HEADS):
            head = 4 * j + 2 * c + e
            slope = jnp.exp2(-(head + 1).astype(_F32))
            bias_ref[var, j] = jnp.where(valid, -slope * distf, NEG)


def _even_kernel(x_ref, gpre_ref, gpost_ref, win_ref, sink_ref, poolw_ref, pscale_ref, wout_ref,
                 o_ref, kst_ref, vst_ref, q_ref, att_ref, u_ref, t1_ref, t2_ref, bias_ref, mask_ref,
                 *, ts):
    s = pl.program_id(1)
    nblk = ts // BLOCK

    @pl.when(s == 0)
    def _start_of_sequence():
        _build_attn_tables(bias_ref, mask_ref)
        kst_ref[:, :, 0:BLOCK, :] = jnp.zeros((KV_HEADS, 2, BLOCK, LANES), _BF16)
        vst_ref[:, :, 0:BLOCK, :] = jnp.zeros((KV_HEADS, 2, BLOCK, LANES), _BF16)
        u_ref[:, 0:POOL_HIST, :] = jnp.zeros((len(POOL_WINDOWS), POOL_HIST, POOL_GC), _F32)
        t1_ref[0:SUBLANES, :] = jnp.zeros((SUBLANES, POOL_GC), _F32)
        t2_ref[0:SUBLANES, :] = jnp.zeros((SUBLANES, POOL_GC), _F32)

    @pl.when(s > 0)
    def _carry_history():
        kst_ref[:, :, 0:BLOCK, :] = kst_ref[:, :, ts:ts + BLOCK, :]
        vst_ref[:, :, 0:BLOCK, :] = vst_ref[:, :, ts:ts + BLOCK, :]
        u_ref[:, SUBLANES:POOL_HIST, :] = u_ref[:, ts + SUBLANES:ts + POOL_HIST, :]

    x = x_ref[0]
    h = _rms(x, gpre_ref[...]).astype(_BF16)
    proj = jnp.dot(h, win_ref[...], preferred_element_type=_F32)

    o_q, o_k, o_v = 0, ATTN_WIDTH, ATTN_WIDTH + KV_WIDTH
    o_ga = o_v + KV_WIDTH
    o_u = o_ga + ATTN_WIDTH
    o_gb = o_u + POOL_WIDTH

    q_ref[...] = (proj[:, o_q:o_q + ATTN_WIDTH] * (HEAD_DIM ** -0.5)).astype(_BF16)

    lane = lax.broadcasted_iota(jnp.int32, (ts, LANES), 1)
    lo = lane < HEAD_DIM
    for src_off, dst in ((o_k, kst_ref), (o_v, vst_ref)):
        t2 = proj[:, src_off:src_off + KV_WIDTH]
        t2r = pltpu.roll(t2, HEAD_DIM, 1)
        dst[0, 0, BLOCK:, :] = jnp.where(lo, t2, 0.0).astype(_BF16)
        dst[0, 1, BLOCK:, :] = jnp.where(lo, 0.0, t2r).astype(_BF16)
        dst[1, 0, BLOCK:, :] = jnp.where(lo, t2r, 0.0).astype(_BF16)
        dst[1, 1, BLOCK:, :] = jnp.where(lo, 0.0, t2).astype(_BF16)

    lane_b = lax.broadcasted_iota(jnp.int32, (2 * BLOCK, LANES), 1)
    ones_even = jnp.where(lane_b < HEAD_DIM, 1.0, 0.0).astype(_BF16)
    ones_odd = jnp.where(lane_b < HEAD_DIM, 0.0, 1.0).astype(_BF16)
    lo_blk = lax.broadcasted_iota(jnp.int32, (BLOCK, LANES), 1) < HEAD_DIM

    def attn_block(n):
        r0 = n * BLOCK
        var = jnp.where(s == 0, 1, 0) if n == 0 else 0
        msk = mask_ref[var]
        for j in range(KV_HEADS):
            qcat = jnp.concatenate(
                [q_ref[pl.ds(r0, BLOCK), (2 * j) * LANES:(2 * j + 1) * LANES],
                 q_ref[pl.ds(r0, BLOCK), (2 * j + 1) * LANES:(2 * j + 2) * LANES]], axis=0)
            kcat = jnp.concatenate(
                [kst_ref[j, 0, pl.ds(r0, 2 * BLOCK), :], kst_ref[j, 1, pl.ds(r0, 2 * BLOCK), :]], axis=0)
            sc = lax.dot_general(qcat, kcat, (((1,), (1,)), ((), ())),
                                 preferred_element_type=_F32)
            sc = sc * msk + bias_ref[var, j]
            p_rows, ex_rows = [], []
            for c in range(2):
                sc_c = sc[c * BLOCK:(c + 1) * BLOCK]
                halves, exs = [], []
                for e in range(2):
                    sink = sink_ref[4 * j + 2 * c + e]
                    sc_h = sc_c[:, e * 2 * BLOCK:(e + 1) * 2 * BLOCK]
                    m = jnp.maximum(jnp.max(sc_h, axis=-1, keepdims=True), sink)
                    halves.append(jnp.exp(sc_h - m))
                    exs.append(jnp.exp(sink - m))
                p_rows.append(jnp.concatenate(halves, axis=1))
                ex_rows.append(jnp.where(lo_blk, exs[0], exs[1]))
            p = jnp.concatenate(p_rows, axis=0).astype(_BF16)
            ex = jnp.concatenate(ex_rows, axis=0)
            vcat = jnp.concatenate(
                [jnp.concatenate([vst_ref[j, 0, pl.ds(r0, 2 * BLOCK), :], ones_even], axis=1),
                 jnp.concatenate([vst_ref[j, 1, pl.ds(r0, 2 * BLOCK), :], ones_odd], axis=1)], axis=0)
            res = jnp.dot(p, vcat, preferred_element_type=_F32)
            o = res[:, 0:LANES] / (ex + res[:, LANES:2 * LANES])
            att_ref[pl.ds(r0, BLOCK), (2 * j) * LANES:(2 * j + 1) * LANES] = o[0:BLOCK]
            att_ref[pl.ds(r0, BLOCK), (2 * j + 1) * LANES:(2 * j + 2) * LANES] = o[BLOCK:2 * BLOCK]

    for n in range(nblk):
        attn_block(n)

    ya = att_ref[...] * _silu(proj[:, o_ga:o_ga + ATTN_WIDTH])

    ext = ts + POOL_HIST - SUBLANES
    t_abs = s * ts + lax.broadcasted_iota(jnp.int32, (ts, POOL_GC), 0)
    yb_parts = []
    for g, w in enumerate(POOL_WINDOWS):
        cols = slice(g * POOL_GC, (g + 1) * POOL_GC)
        u_ref[g, POOL_HIST:, :] = proj[:, o_u + g * POOL_GC:o_u + (g + 1) * POOL_GC]
        src = u_ref.at[g]
        span = 1
        bufs = (t1_ref, t2_ref)
        k = 0
        while span < w:
            cur = src[pl.ds(SUBLANES, ext), :] + src[pl.ds(SUBLANES - span, ext, stride=1), :]
            span *= 2
            if span < w:
                bufs[k % 2][pl.ds(SUBLANES, ext), :] = cur
                src = bufs[k % 2]
                k += 1
        win_sum = cur[POOL_HIST - SUBLANES:, :]
        cnt = jnp.minimum(t_abs + 1, w).astype(_F32)
        u_g = u_ref[g, pl.ds(POOL_HIST, ts), :]
        pooled = (win_sum / cnt - u_g).astype(_BF16)
        y_g = jnp.dot(pooled, poolw_ref[g], preferred_element_type=_F32)
        yb_parts.append(y_g * pscale_ref[:, cols])
    yb = jnp.concatenate(yb_parts, axis=1) * _silu(proj[:, o_gb:o_gb + POOL_WIDTH])

    y = jnp.concatenate([ya, yb], axis=1).astype(_BF16)
    mix = jnp.dot(y, wout_ref[...], preferred_element_type=_F32)
    o_ref[0] = x + _rms(mix, gpost_ref[...])


def _odd_kernel(x_ref, gpre_ref, gpost_ref, win_ref, dww_ref, dwb_ref, lng_ref, lnb_ref, wout_ref,
                o_ref, g_ref, cf_ref, *, ts):
    s = pl.program_id(1)

    @pl.when(s == 0)
    def _start_of_sequence():
        g_ref[:, 0:CONV_HIST, :] = jnp.zeros((N_SLAB, CONV_HIST, LANES), _F32)

    @pl.when(s > 0)
    def _carry_history():
        g_ref[:, 0:CONV_HIST, :] = g_ref[:, ts:ts + CONV_HIST, :]

    x = x_ref[0]
    h = _rms(x, gpre_ref[...]).astype(_BF16)

    first = CONV_HIST - (CONV_K - 1)
    for lc in range(N_SLAB):
        cols = slice(lc * LANES, (lc + 1) * LANES)
        ab = jnp.dot(h, win_ref[:, 2 * lc * LANES:2 * (lc + 1) * LANES], preferred_element_type=_F32)
        g_ref[lc, CONV_HIST:, :] = ab[:, 0:LANES] * jax.nn.sigmoid(ab[:, LANES:2 * LANES])
        for rc in range(ts // CONV_ROWS):
            base = rc * CONV_ROWS
            acc = jnp.zeros((CONV_ROWS, LANES), _F32)
            for j in range(CONV_K):
                acc = acc + dww_ref[j:j + 1, cols] * g_ref[lc, pl.ds(base + first + j, CONV_ROWS, stride=1), :]
            cf_ref[base:base + CONV_ROWS, cols] = acc + dwb_ref[:, cols]

    gate = jnp.dot(h, win_ref[:, 2 * D_MODEL:3 * D_MODEL], preferred_element_type=_F32)
    cf = cf_ref[...]
    mu = jnp.mean(cf, axis=-1, keepdims=True)
    d = cf - mu
    var = jnp.mean(d * d, axis=-1, keepdims=True)
    cn = d * lax.rsqrt(var + EPS) * lng_ref[...] + lnb_ref[...]
    y = (_silu(cn) * _silu(gate)).astype(_BF16)
    mix = jnp.dot(y, wout_ref[...], preferred_element_type=_F32)
    o_ref[0] = x + _rms(mix, gpost_ref[...])


def _full(shape):
    return pl.BlockSpec(shape, lambda b, s: (0,) * len(shape))


def _compiler_params():
    return pltpu.CompilerParams(dimension_semantics=("arbitrary", "arbitrary"),
                                vmem_limit_bytes=VMEM_LIMIT_BYTES)


def _even_layer(x, gpre, gpost, w_in, sinks, pool_w, pool_scale, w_out, ts):
    b, seq, d = x.shape
    n_pool = len(POOL_WINDOWS)
    tile = pl.BlockSpec((1, ts, d), lambda i, j: (i, j, 0))
    return pl.pallas_call(
        functools.partial(_even_kernel, ts=ts),
        out_shape=jax.ShapeDtypeStruct(x.shape, x.dtype),
        grid=(b, seq // ts),
        in_specs=[tile, _full((1, d)), _full((1, d)), _full((d, EVEN_IN)),
                  pl.BlockSpec(memory_space=pltpu.SMEM),
                  _full((n_pool, POOL_GC, POOL_GC)), _full((1, POOL_WIDTH)),
                  _full((d, d))],
        out_specs=tile,
        scratch_shapes=[
            pltpu.VMEM((KV_HEADS, 2, ts + BLOCK, LANES), _BF16),
            pltpu.VMEM((KV_HEADS, 2, ts + BLOCK, LANES), _BF16),
            pltpu.VMEM((ts, ATTN_WIDTH), _BF16),
            pltpu.VMEM((ts, ATTN_WIDTH), _F32),
            pltpu.VMEM((n_pool, ts + POOL_HIST, POOL_GC), _F32),
            pltpu.VMEM((ts + POOL_HIST, POOL_GC), _F32),
            pltpu.VMEM((ts + POOL_HIST, POOL_GC), _F32),
            pltpu.VMEM((2, KV_HEADS, 2 * BLOCK, 4 * BLOCK), _F32),
            pltpu.VMEM((2, 2 * BLOCK, 4 * BLOCK), _F32),
        ],
        compiler_params=_compiler_params(),
        name="even_layer",
    )(x, gpre, gpost, w_in, sinks, pool_w, pool_scale, w_out)


def _odd_layer(x, gpre, gpost, w_in, dw_w, dw_b, ln_g, ln_b, w_out, ts):
    b, seq, d = x.shape
    tile = pl.BlockSpec((1, ts, d), lambda i, j: (i, j, 0))
    return pl.pallas_call(
        functools.partial(_odd_kernel, ts=ts),
        out_shape=jax.ShapeDtypeStruct(x.shape, x.dtype),
        grid=(b, seq // ts),
        in_specs=[tile, _full((1, d)), _full((1, d)), _full((d, ODD_IN)), _full((CONV_K, d)),
                  _full((1, d)), _full((1, d)), _full((1, d)), _full((d, d))],
        out_specs=tile,
        scratch_shapes=[
            pltpu.VMEM((N_SLAB, ts + CONV_HIST, LANES), _F32),
            pltpu.VMEM((ts, d), _F32),
        ],
        compiler_params=_compiler_params(),
        name="odd_layer",
    )(x, gpre, gpost, w_in, dw_w, dw_b, ln_g, ln_b, w_out)


def _interleave_glu_columns(w_in):
    d = w_in.shape[0]
    a = w_in[:, 0:D_MODEL].reshape(d, N_SLAB, 1, LANES)
    b = w_in[:, D_MODEL:2 * D_MODEL].reshape(d, N_SLAB, 1, LANES)
    ab = jnp.concatenate([a, b], axis=2).reshape(d, 2 * D_MODEL)
    return jnp.concatenate([ab, w_in[:, 2 * D_MODEL:]], axis=1)


def kernel(x, pre_norm, post_norm, a_w_in, a_sinks, b_pool_w, b_pool_scale, ab_w_out,
           c_w_in, c_dw_w, c_dw_b, c_ln_g, c_ln_b, c_w_out):
    depth = pre_norm.shape[0]
    ts = SEQ_TILE
    assert x.shape[1] % ts == 0 and ts % BLOCK == 0 and ts % CONV_ROWS == 0
    row = lambda v: v.reshape(1, -1).astype(_F32)
    for layer in range(depth):
        i = layer // 2
        if layer % 2 == 0:
            x = _even_layer(x, row(pre_norm[layer]), row(post_norm[layer]),
                            a_w_in[i].astype(_BF16), a_sinks[i].astype(_F32),
                            b_pool_w[i].astype(_BF16), row(b_pool_scale[i]),
                            ab_w_out[i].astype(_BF16), ts)
        else:
            x = _odd_layer(x, row(pre_norm[layer]), row(post_norm[layer]),
                           _interleave_glu_columns(c_w_in[i]).astype(_BF16),
                           c_dw_w[i].reshape(CONV_K, -1).astype(_F32),
                           row(c_dw_b[i]), row(c_ln_g[i]), row(c_ln_b[i]),
                           c_w_out[i].astype(_BF16), ts)
    return x
```

```python
import functools

import jax
import jax.numpy as jnp
from jax import lax
from jax.experimental import pallas as pl
from jax.experimental.pallas import tpu as pltpu

D_MODEL = 1024
HEAD_DIM = 64
Q_HEADS = 8
KV_HEADS = 2
ATTN_WIDTH = Q_HEADS * HEAD_DIM
KV_WIDTH = KV_HEADS * HEAD_DIM
BLOCK = 128
POOL_WIDTH = D_MODEL - ATTN_WIDTH
POOL_WINDOWS = (2, 4, 8, 16)
POOL_GC = POOL_WIDTH // len(POOL_WINDOWS)
EVEN_IN = ATTN_WIDTH + 2 * KV_WIDTH + ATTN_WIDTH + POOL_WIDTH + POOL_WIDTH
CONV_K = 31
ODD_IN = 3 * D_MODEL
EPS = 1e-6
NEG = -1e30

LANES = 128
SUBLANES = 8
N_SLAB = D_MODEL // LANES
SEQ_TILE = 1024
POOL_HIST = 24
CONV_HIST = 32
CONV_ROWS = 64
VMEM_LIMIT_BYTES = 56 * 1024 * 1024

_F32 = jnp.float32
_BF16 = jnp.bfloat16


def _rms(xf, g):
    return xf * lax.rsqrt(jnp.mean(xf * xf, axis=-1, keepdims=True) + EPS) * g


def _silu(v):
    return v * jax.nn.sigmoid(v)


def _build_attn_tables(bias_ref, mask_ref):
    shape = (2 * BLOCK, 4 * BLOCK)
    r = lax.broadcasted_iota(jnp.int32, shape, 0)
    col = lax.broadcasted_iota(jnp.int32, shape, 1)
    c = r >> 7
    i = r & (BLOCK - 1)
    e = col >> 8
    sj = col & (2 * BLOCK - 1)
    dist = i + BLOCK - sj
    valid0 = (dist >= 0) & (dist < BLOCK)
    distf = dist.astype(_F32)
    for var in range(2):
        valid = valid0 & (sj >= BLOCK) if var == 1 else valid0
        mask_ref[var] = jnp.where(valid, 1.0, 0.0).astype(_F32)
        for j in range(KV_HEADS):
            head = 4 * j + 2 * c + e
            slope = jnp.exp2(-(head + 1).astype(_F32))
            bias_ref[var, j] = jnp.where(valid, -slope * distf, NEG)


def _even_kernel(x_ref, gpre_ref, gpost_ref, win_ref, sink_ref, poolw_ref, pscale_ref, wout_ref,
                 o_ref, kst_ref, vst_ref, q_ref, att_ref, u_ref, t1_ref, t2_ref, bias_ref, mask_ref,
                 *, ts):
    s = pl.program_id(1)
    nblk = ts // BLOCK

    @pl.when(s == 0)
    def _start_of_sequence():
        _build_attn_tables(bias_ref, mask_ref)
        kst_ref[:, :, 0:BLOCK, :] = jnp.zeros((KV_HEADS, 2, BLOCK, LANES), _BF16)
        vst_ref[:, :, 0:BLOCK, :] = jnp.zeros((KV_HEADS, 2, BLOCK, LANES), _BF16)
        u_ref[:, 0:POOL_HIST, :] = jnp.zeros((len(POOL_WINDOWS), POOL_HIST, POOL_GC), _F32)
        t1_ref[0:SUBLANES, :] = jnp.zeros((SUBLANES, POOL_GC), _F32)
        t2_ref[0:SUBLANES, :] = jnp.zeros((SUBLANES, POOL_GC), _F32)

    @pl.when(s > 0)
    def _carry_history():
        kst_ref[:, :, 0:BLOCK, :] = kst_ref[:, :, ts:ts + BLOCK, :]
        vst_ref[:, :, 0:BLOCK, :] = vst_ref[:, :, ts:ts + BLOCK, :]
        u_ref[:, SUBLANES:POOL_HIST, :] = u_ref[:, ts + SUBLANES:ts + POOL_HIST, :]

    x = x_ref[0]
    h = _rms(x, gpre_ref[...]).astype(_BF16)
    proj = jnp.dot(h, win_ref[...], preferred_element_type=_F32)

    o_q, o_k, o_v = 0, ATTN_WIDTH, ATTN_WIDTH + KV_WIDTH
    o_ga = o_v + KV_WIDTH
    o_u = o_ga + ATTN_WIDTH
    o_gb = o_u + POOL_WIDTH

    q_ref[...] = (proj[:, o_q:o_q + ATTN_WIDTH] * (HEAD_DIM ** -0.5)).astype(_BF16)

    lane = lax.broadcasted_iota(jnp.int32, (ts, LANES), 1)
    lo = lane < HEAD_DIM
    for src_off, dst in ((o_k, kst_ref), (o_v, vst_ref)):
        t2 = proj[:, src_off:src_off + KV_WIDTH]
        t2r = pltpu.roll(t2, HEAD_DIM, 1)
        dst[0, 0, BLOCK:, :] = jnp.where(lo, t2, 0.0).astype(_BF16)
        dst[0, 1, BLOCK:, :] = jnp.where(lo, 0.0, t2r).astype(_BF16)
        dst[1, 0, BLOCK:, :] = jnp.where(lo, t2r, 0.0).astype(_BF16)
        dst[1, 1, BLOCK:, :] = jnp.where(lo, 0.0, t2).astype(_BF16)

    lane_b = lax.broadcasted_iota(jnp.int32, (2 * BLOCK, LANES), 1)
    ones_even = jnp.where(lane_b < HEAD_DIM, 1.0, 0.0).astype(_BF16)
    ones_odd = jnp.where(lane_b < HEAD_DIM, 0.0, 1.0).astype(_BF16)
    lo_blk = lax.broadcasted_iota(jnp.int32, (BLOCK, LANES), 1) < HEAD_DIM

    def attn_block(n):
        r0 = n * BLOCK
        var = jnp.where(s == 0, 1, 0) if n == 0 else 0
        msk = mask_ref[var]
        for j in range(KV_HEADS):
            qcat = jnp.concatenate(
                [q_ref[pl.ds(r0, BLOCK), (2 * j) * LANES:(2 * j + 1) * LANES],
                 q_ref[pl.ds(r0, BLOCK), (2 * j + 1) * LANES:(2 * j + 2) * LANES]], axis=0)
            kcat = jnp.concatenate(
                [kst_ref[j, 0, pl.ds(r0, 2 * BLOCK), :], kst_ref[j, 1, pl.ds(r0, 2 * BLOCK), :]], axis=0)
            sc = lax.dot_general(qcat, kcat, (((1,), (1,)), ((), ())),
                                 preferred_element_type=_F32)
            sc = sc * msk + bias_ref[var, j]
            p_rows, ex_rows = [], []
            for c in range(2):
                sc_c = sc[c * BLOCK:(c + 1) * BLOCK]
                halves, exs = [], []
                for e in range(2):
                    sink = sink_ref[4 * j + 2 * c + e]
                    sc_h = sc_c[:, e * 2 * BLOCK:(e + 1) * 2 * BLOCK]
                    m = jnp.maximum(jnp.max(sc_h, axis=-1, keepdims=True), sink)
                    halves.append(jnp.exp(sc_h - m))
                    exs.append(jnp.exp(sink - m))
                p_rows.append(jnp.concatenate(halves, axis=1))
                ex_rows.append(jnp.where(lo_blk, exs[0], exs[1]))
            p = jnp.concatenate(p_rows, axis=0).astype(_BF16)
            ex = jnp.concatenate(ex_rows, axis=0)
            vcat = jnp.concatenate(
                [jnp.concatenate([vst_ref[j, 0, pl.ds(r0, 2 * BLOCK), :], ones_even], axis=1),
                 jnp.concatenate([vst_ref[j, 1, pl.ds(r0, 2 * BLOCK), :], ones_odd], axis=1)], axis=0)
            res = jnp.dot(p, vcat, preferred_element_type=_F32)
            o = res[:, 0:LANES] / (ex + res[:, LANES:2 * LANES])
            att_ref[pl.ds(r0, BLOCK), (2 * j) * LANES:(2 * j + 1) * LANES] = o[0:BLOCK]
            att_ref[pl.ds(r0, BLOCK), (2 * j + 1) * LANES:(2 * j + 2) * LANES] = o[BLOCK:2 * BLOCK]

    for n in range(nblk):
        attn_block(n)

    ya = att_ref[...] * _silu(proj[:, o_ga:o_ga + ATTN_WIDTH])

    ext = ts + POOL_HIST - SUBLANES
    t_abs = s * ts + lax.broadcasted_iota(jnp.int32, (ts, POOL_GC), 0)
    yb_parts = []
    for g, w in enumerate(POOL_WINDOWS):
        cols = slice(g * POOL_GC, (g + 1) * POOL_GC)
        u_ref[g, POOL_HIST:, :] = proj[:, o_u + g * POOL_GC:o_u + (g + 1) * POOL_GC]
        src = u_ref.at[g]
        span = 1
        bufs = (t1_ref, t2_ref)
        k = 0
        while span < w:
            cur = src[pl.ds(SUBLANES, ext), :] + src[pl.ds(SUBLANES - span, ext, stride=1), :]
            span *= 2
            if span < w:
                bufs[k % 2][pl.ds(SUBLANES, ext), :] = cur
                src = bufs[k % 2]
                k += 1
        win_sum = cur[POOL_HIST - SUBLANES:, :]
        cnt = jnp.minimum(t_abs + 1, w).astype(_F32)
        u_g = u_ref[g, pl.ds(POOL_HIST, ts), :]
        pooled = (win_sum / cnt - u_g).astype(_BF16)
        y_g = jnp.dot(pooled, poolw_ref[g], preferred_element_type=_F32)
        yb_parts.append(y_g * pscale_ref[:, cols])
    yb = jnp.concatenate(yb_parts, axis=1) * _silu(proj[:, o_gb:o_gb + POOL_WIDTH])

    y = jnp.concatenate([ya, yb], axis=1).astype(_BF16)
    mix = jnp.dot(y, wout_ref[...], preferred_element_type=_F32)
    o_ref[0] = x + _rms(mix, gpost_ref[...])


def _odd_kernel(x_ref, gpre_ref, gpost_ref, win_ref, dww_ref, dwb_ref, lng_ref, lnb_ref, wout_ref,
                o_ref, g_ref, cf_ref, *, ts):
    s = pl.program_id(1)

    @pl.when(s == 0)
    def _start_of_sequence():
        g_ref[:, 0:CONV_HIST, :] = jnp.zeros((N_SLAB, CONV_HIST, LANES), _F32)

    @pl.when(s > 0)
    def _carry_history():
        g_ref[:, 0:CONV_HIST, :] = g_ref[:, ts:ts + CONV_HIST, :]

    x = x_ref[0]
    h = _rms(x, gpre_ref[...]).astype(_BF16)

    first = CONV_HIST - (CONV_K - 1)
    for lc in range(N_SLAB):
        cols = slice(lc * LANES, (lc + 1) * LANES)
        ab = jnp.dot(h, win_ref[:, 2 * lc * LANES:2 * (lc + 1) * LANES], preferred_element_type=_F32)
        g_ref[lc, CONV_HIST:, :] = ab[:, 0:LANES] * jax.nn.sigmoid(ab[:, LANES:2 * LANES])
        for rc in range(ts // CONV_ROWS):
            base = rc * CONV_ROWS
            acc = jnp.zeros((CONV_ROWS, LANES), _F32)
            for j in range(CONV_K):
                acc = acc + dww_ref[j:j + 1, cols] * g_ref[lc, pl.ds(base + first + j, CONV_ROWS, stride=1), :]
            cf_ref[base:base + CONV_ROWS, cols] = acc + dwb_ref[:, cols]

    gate = jnp.dot(h, win_ref[:, 2 * D_MODEL:3 * D_MODEL], preferred_element_type=_F32)
    cf = cf_ref[...]
    mu = jnp.mean(cf, axis=-1, keepdims=True)
    d = cf - mu
    var = jnp.mean(d * d, axis=-1, keepdims=True)
    cn = d * lax.rsqrt(var + EPS) * lng_ref[...] + lnb_ref[...]
    y = (_silu(cn) * _silu(gate)).astype(_BF16)
    mix = jnp.dot(y, wout_ref[...], preferred_element_type=_F32)
    o_ref[0] = x + _rms(mix, gpost_ref[...])


def _full(shape):
    return pl.BlockSpec(shape, lambda b, s: (0,) * len(shape))


def _compiler_params():
    return pltpu.CompilerParams(dimension_semantics=("arbitrary", "arbitrary"),
                                vmem_limit_bytes=VMEM_LIMIT_BYTES)


def _even_layer(x, gpre, gpost, w_in, sinks, pool_w, pool_scale, w_out, ts):
    b, seq, d = x.shape
    n_pool = len(POOL_WINDOWS)
    tile = pl.BlockSpec((1, ts, d), lambda i, j: (i, j, 0))
    return pl.pallas_call(
        functools.partial(_even_kernel, ts=ts),
        out_shape=jax.ShapeDtypeStruct(x.shape, x.dtype),
        grid=(b, seq // ts),
        in_specs=[tile, _full((1, d)), _full((1, d)), _full((d, EVEN_IN)),
                  pl.BlockSpec(memory_space=pltpu.SMEM),
                  _full((n_pool, POOL_GC, POOL_GC)), _full((1, POOL_WIDTH)),
                  _full((d, d))],
        out_specs=tile,
        scratch_shapes=[
            pltpu.VMEM((KV_HEADS, 2, ts + BLOCK, LANES), _BF16),
            pltpu.VMEM((KV_HEADS, 2, ts + BLOCK, LANES), _BF16),
            pltpu.VMEM((ts, ATTN_WIDTH), _BF16),
            pltpu.VMEM((ts, ATTN_WIDTH), _F32),
            pltpu.VMEM((n_pool, ts + POOL_HIST, POOL_GC), _F32),
            pltpu.VMEM((ts + POOL_HIST, POOL_GC), _F32),
            pltpu.VMEM((ts + POOL_HIST, POOL_GC), _F32),
            pltpu.VMEM((2, KV_HEADS, 2 * BLOCK, 4 * BLOCK), _F32),
            pltpu.VMEM((2, 2 * BLOCK, 4 * BLOCK), _F32),
        ],
        compiler_params=_compiler_params(),
        name="even_layer",
    )(x, gpre, gpost, w_in, sinks, pool_w, pool_scale, w_out)


def _odd_layer(x, gpre, gpost, w_in, dw_w, dw_b, ln_g, ln_b, w_out, ts):
    b, seq, d = x.shape
    tile = pl.BlockSpec((1, ts, d), lambda i, j: (i, j, 0))
    return pl.pallas_call(
        functools.partial(_odd_kernel, ts=ts),
        out_shape=jax.ShapeDtypeStruct(x.shape, x.dtype),
        grid=(b, seq // ts),
        in_specs=[tile, _full((1, d)), _full((1, d)), _full((d, ODD_IN)), _full((CONV_K, d)),
                  _full((1, d)), _full((1, d)), _full((1, d)), _full((d, d))],
        out_specs=tile,
        scratch_shapes=[
            pltpu.VMEM((N_SLAB, ts + CONV_HIST, LANES), _F32),
            pltpu.VMEM((ts, d), _F32),
        ],
        compiler_params=_compiler_params(),
        name="odd_layer",
    )(x, gpre, gpost, w_in, dw_w, dw_b, ln_g, ln_b, w_out)


def _interleave_glu_columns(w_in):
    d = w_in.shape[0]
    a = w_in[:, 0:D_MODEL].reshape(d, N_SLAB, 1, LANES)
    b = w_in[:, D_MODEL:2 * D_MODEL].reshape(d, N_SLAB, 1, LANES)
    ab = jnp.concatenate([a, b], axis=2).reshape(d, 2 * D_MODEL)
    return jnp.concatenate([ab, w_in[:, 2 * D_MODEL:]], axis=1)


def kernel(x, pre_norm, post_norm, a_w_in, a_sinks, b_pool_w, b_pool_scale, ab_w_out,
           c_w_in, c_dw_w, c_dw_b, c_ln_g, c_ln_b, c_w_out):
    depth = pre_norm.shape[0]
    ts = SEQ_TILE
    assert x.shape[1] % ts == 0 and ts % BLOCK == 0 and ts % CONV_ROWS == 0
    row = lambda v: v.reshape(1, -1).astype(_F32)
    for layer in range(depth):
        i = layer // 2
        if layer % 2 == 0:
            x = _even_layer(x, row(pre_norm[layer]), row(post_norm[layer]),
                            a_w_in[i].astype(_BF16), a_sinks[i].astype(_F32),
                            b_pool_w[i].astype(_BF16), row(b_pool_scale[i]),
                            ab_w_out[i].astype(_BF16), ts)
        else:
            x = _odd_layer(x, row(pre_norm[layer]), row(post_norm[layer]),
                           _interleave_glu_columns(c_w_in[i]).astype(_BF16),
                           c_dw_w[i].reshape(CONV_K, -1).astype(_F32),
                           row(c_dw_b[i]), row(c_ln_g[i]), row(c_ln_b[i]),
                           c_w_out[i].astype(_BF16), ts)
    return x
```

```python
import functools

import jax
import jax.numpy as jnp
from jax import lax
from jax.experimental import pallas as pl
from jax.experimental.pallas import tpu as pltpu

D_MODEL = 1024
HEAD_DIM = 64
Q_HEADS = 8
KV_HEADS = 2
ATTN_WIDTH = Q_HEADS * HEAD_DIM
KV_WIDTH = KV_HEADS * HEAD_DIM
BLOCK = 128
POOL_WIDTH = D_MODEL - ATTN_WIDTH
POOL_WINDOWS = (2, 4, 8, 16)
POOL_GC = POOL_WIDTH // len(POOL_WINDOWS)
EVEN_IN = ATTN_WIDTH + 2 * KV_WIDTH + ATTN_WIDTH + POOL_WIDTH + POOL_WIDTH
CONV_K = 31
ODD_IN = 3 * D_MODEL
EPS = 1e-6
NEG = -1e30
LOG2E = 1.4426950408889634

LANES = 128
SUBLANES = 8
N_SLAB = D_MODEL // LANES
SEQ_TILE = 1024
POOL_HIST = 24
CONV_HIST = 32
CONV_ROWS = 64
VMEM_LIMIT_BYTES = 56 * 1024 * 1024

_F32 = jnp.float32
_BF16 = jnp.bfloat16


def _rms(xf, g):
    return xf * lax.rsqrt(jnp.mean(xf * xf, axis=-1, keepdims=True) + EPS) * g


def _silu(v):
    return v * jax.nn.sigmoid(v)


def _build_attn_tables(bias_ref, mask_ref):
    shape = (2 * BLOCK, 4 * BLOCK)
    r = lax.broadcasted_iota(jnp.int32, shape, 0)
    col = lax.broadcasted_iota(jnp.int32, shape, 1)
    c = r >> 7
    i = r & (BLOCK - 1)
    e = col >> 8
    sj = col & (2 * BLOCK - 1)
    dist = i + BLOCK - sj
    valid0 = (dist >= 0) & (dist < BLOCK)
    distf = dist.astype(_F32)
    for var in range(2):
        valid = valid0 & (sj >= BLOCK) if var == 1 else valid0
        mask_ref[var] = jnp.where(valid, 1.0, 0.0).astype(_F32)
        for j in range(KV_HEADS):
            head = 4 * j + 2 * c + e
            slope = jnp.exp2(-(head + 1).astype(_F32))
            bias_ref[var, j] = jnp.where(valid, -slope * LOG2E * distf, NEG)


def _even_kernel(x_ref, gpre_ref, gpost_ref, win_ref, sink_ref, poolw_ref, pscale_ref, wout_ref,
                 o_ref, kst_ref, vst_ref, q_ref, att_ref, u_ref, t1_ref, t2_ref, bias_ref, mask_ref,
                 *, ts):
    s = pl.program_id(1)
    nblk = ts // BLOCK

    @pl.when(s == 0)
    def _start_of_sequence():
        _build_attn_tables(bias_ref, mask_ref)
        kst_ref[:, :, 0:BLOCK, :] = jnp.zeros((KV_HEADS, 2, BLOCK, LANES), _BF16)
        vst_ref[:, :, 0:BLOCK, :] = jnp.zeros((KV_HEADS, 2, BLOCK, LANES), _BF16)
        u_ref[:, 0:POOL_HIST, :] = jnp.zeros((len(POOL_WINDOWS), POOL_HIST, POOL_GC), _F32)
        t1_ref[0:SUBLANES, :] = jnp.zeros((SUBLANES, POOL_GC), _F32)
        t2_ref[0:SUBLANES, :] = jnp.zeros((SUBLANES, POOL_GC), _F32)

    @pl.when(s > 0)
    def _carry_history():
        kst_ref[:, :, 0:BLOCK, :] = kst_ref[:, :, ts:ts + BLOCK, :]
        vst_ref[:, :, 0:BLOCK, :] = vst_ref[:, :, ts:ts + BLOCK, :]
        u_ref[:, SUBLANES:POOL_HIST, :] = u_ref[:, ts + SUBLANES:ts + POOL_HIST, :]

    x = x_ref[0]
    h = _rms(x, gpre_ref[...]).astype(_BF16)
    proj = jnp.dot(h, win_ref[...], preferred_element_type=_F32)

    o_q, o_k, o_v = 0, ATTN_WIDTH, ATTN_WIDTH + KV_WIDTH
    o_ga = o_v + KV_WIDTH
    o_u = o_ga + ATTN_WIDTH
    o_gb = o_u + POOL_WIDTH

    q_ref[...] = (proj[:, o_q:o_q + ATTN_WIDTH] * (HEAD_DIM ** -0.5 * LOG2E)).astype(_BF16)

    lane = lax.broadcasted_iota(jnp.int32, (ts, LANES), 1)
    lo = lane < HEAD_DIM
    for src_off, dst in ((o_k, kst_ref), (o_v, vst_ref)):
        t2 = proj[:, src_off:src_off + KV_WIDTH]
        t2r = pltpu.roll(t2, HEAD_DIM, 1)
        dst[0, 0, BLOCK:, :] = jnp.where(lo, t2, 0.0).astype(_BF16)
        dst[0, 1, BLOCK:, :] = jnp.where(lo, 0.0, t2r).astype(_BF16)
        dst[1, 0, BLOCK:, :] = jnp.where(lo, t2r, 0.0).astype(_BF16)
        dst[1, 1, BLOCK:, :] = jnp.where(lo, 0.0, t2).astype(_BF16)

    lane_b = lax.broadcasted_iota(jnp.int32, (2 * BLOCK, LANES), 1)
    ones_even = jnp.where(lane_b < HEAD_DIM, 1.0, 0.0).astype(_BF16)
    ones_odd = jnp.where(lane_b < HEAD_DIM, 0.0, 1.0).astype(_BF16)
    lo_blk = lax.broadcasted_iota(jnp.int32, (BLOCK, LANES), 1) < HEAD_DIM

    def attn_block(n):
        r0 = n * BLOCK
        var = jnp.where(s == 0, 1, 0) if n == 0 else 0
        msk = mask_ref[var]
        for j in range(KV_HEADS):
            qcat = jnp.concatenate(
                [q_ref[pl.ds(r0, BLOCK), (2 * j) * LANES:(2 * j + 1) * LANES],
                 q_ref[pl.ds(r0, BLOCK), (2 * j + 1) * LANES:(2 * j + 2) * LANES]], axis=0)
            kcat = jnp.concatenate(
                [kst_ref[j, 0, pl.ds(r0, 2 * BLOCK), :], kst_ref[j, 1, pl.ds(r0, 2 * BLOCK), :]], axis=0)
            sc = lax.dot_general(qcat, kcat, (((1,), (1,)), ((), ())),
                                 preferred_element_type=_F32)
            sc = sc * msk + bias_ref[var, j]
            p_rows, ex_rows = [], []
            for c in range(2):
                sc_c = sc[c * BLOCK:(c + 1) * BLOCK]
                halves, exs = [], []
                for e in range(2):
                    sink = sink_ref[4 * j + 2 * c + e] * LOG2E
                    sc_h = sc_c[:, e * 2 * BLOCK:(e + 1) * 2 * BLOCK]
                    m = jnp.maximum(jnp.max(sc_h, axis=-1, keepdims=True), sink)
                    halves.append(jnp.exp2(sc_h - m))
                    exs.append(jnp.exp2(sink - m))
                p_rows.append(jnp.concatenate(halves, axis=1))
                ex_rows.append(jnp.where(lo_blk, exs[0], exs[1]))
            p = jnp.concatenate(p_rows, axis=0).astype(_BF16)
            ex = jnp.concatenate(ex_rows, axis=0)
            vcat = jnp.concatenate(
                [jnp.concatenate([vst_ref[j, 0, pl.ds(r0, 2 * BLOCK), :], ones_even], axis=1),
                 jnp.concatenate([vst_ref[j, 1, pl.ds(r0, 2 * BLOCK), :], ones_odd], axis=1)], axis=0)
            res = jnp.dot(p, vcat, preferred_element_type=_F32)
            o = res[:, 0:LANES] / (ex + res[:, LANES:2 * LANES])
            att_ref[pl.ds(r0, BLOCK), (2 * j) * LANES:(2 * j + 1) * LANES] = o[0:BLOCK]
            att_ref[pl.ds(r0, BLOCK), (2 * j + 1) * LANES:(2 * j + 2) * LANES] = o[BLOCK:2 * BLOCK]

    for n in range(nblk):
        attn_block(n)

    ya = att_ref[...] * _silu(proj[:, o_ga:o_ga + ATTN_WIDTH])

    ext = ts + POOL_HIST - SUBLANES
    t_abs = s * ts + lax.broadcasted_iota(jnp.int32, (ts, POOL_GC), 0)
    yb_parts = []
    for g, w in enumerate(POOL_WINDOWS):
        cols = slice(g * POOL_GC, (g + 1) * POOL_GC)
        u_ref[g, POOL_HIST:, :] = proj[:, o_u + g * POOL_GC:o_u + (g + 1) * POOL_GC]
        src = u_ref.at[g]
        span = 1
        bufs = (t1_ref, t2_ref)
        k = 0
        while span < w:
            cur = src[pl.ds(SUBLANES, ext), :] + src[pl.ds(SUBLANES - span, ext, stride=1), :]
            span *= 2
            if span < w:
                bufs[k % 2][pl.ds(SUBLANES, ext), :] = cur
                src = bufs[k % 2]
                k += 1
        win_sum = cur[POOL_HIST - SUBLANES:, :]
        cnt = jnp.minimum(t_abs + 1, w).astype(_F32)
        u_g = u_ref[g, pl.ds(POOL_HIST, ts), :]
        pooled = (win_sum / cnt - u_g).astype(_BF16)
        y_g = jnp.dot(pooled, poolw_ref[g], preferred_element_type=_F32)
        yb_parts.append(y_g * pscale_ref[:, cols])
    yb = jnp.concatenate(yb_parts, axis=1) * _silu(proj[:, o_gb:o_gb + POOL_WIDTH])

    y = jnp.concatenate([ya, yb], axis=1).astype(_BF16)
    mix = jnp.dot(y, wout_ref[...], preferred_element_type=_F32)
    o_ref[0] = x + _rms(mix, gpost_ref[...])


def _odd_kernel(x_ref, gpre_ref, gpost_ref, win_ref, dww_ref, dwb_ref, lng_ref, lnb_ref, wout_ref,
                o_ref, g_ref, cf_ref, *, ts):
    s = pl.program_id(1)

    @pl.when(s == 0)
    def _start_of_sequence():
        g_ref[:, 0:CONV_HIST, :] = jnp.zeros((N_SLAB, CONV_HIST, LANES), _F32)

    @pl.when(s > 0)
    def _carry_history():
        g_ref[:, 0:CONV_HIST, :] = g_ref[:, ts:ts + CONV_HIST, :]

    x = x_ref[0]
    h = _rms(x, gpre_ref[...]).astype(_BF16)

    first = CONV_HIST - (CONV_K - 1)
    pair_w = 2 * LANES
    for pair in range(N_SLAB // 2):
        a2 = jnp.dot(h, win_ref[:, pair * pair_w:(pair + 1) * pair_w], preferred_element_type=_F32)
        b2 = jnp.dot(h, win_ref[:, D_MODEL + pair * pair_w:D_MODEL + (pair + 1) * pair_w],
                     preferred_element_type=_F32)
        for half in range(2):
            hl = slice(half * LANES, (half + 1) * LANES)
            g_ref[2 * pair + half, CONV_HIST:, :] = a2[:, hl] * jax.nn.sigmoid(b2[:, hl])
        for lc in (2 * pair, 2 * pair + 1):
            cols = slice(lc * LANES, (lc + 1) * LANES)
            for rc in range(ts // CONV_ROWS):
                base = rc * CONV_ROWS
                acc = jnp.zeros((CONV_ROWS, LANES), _F32)
                for j in range(CONV_K):
                    acc = acc + dww_ref[j:j + 1, cols] * g_ref[lc, pl.ds(base + first + j, CONV_ROWS, stride=1), :]
                cf_ref[base:base + CONV_ROWS, cols] = acc + dwb_ref[:, cols]

    gate = jnp.dot(h, win_ref[:, 2 * D_MODEL:3 * D_MODEL], preferred_element_type=_F32)
    cf = cf_ref[...]
    mu = jnp.mean(cf, axis=-1, keepdims=True)
    d = cf - mu
    var = jnp.mean(d * d, axis=-1, keepdims=True)
    cn = d * lax.rsqrt(var + EPS) * lng_ref[...] + lnb_ref[...]
    y = (_silu(cn) * _silu(gate)).astype(_BF16)
    mix = jnp.dot(y, wout_ref[...], preferred_element_type=_F32)
    o_ref[0] = x + _rms(mix, gpost_ref[...])


def _full(shape):
    return pl.BlockSpec(shape, lambda b, s: (0,) * len(shape))


def _compiler_params():
    return pltpu.CompilerParams(dimension_semantics=("arbitrary", "arbitrary"),
                                vmem_limit_bytes=VMEM_LIMIT_BYTES)


def _even_layer(x, gpre, gpost, w_in, sinks, pool_w, pool_scale, w_out, ts):
    b, seq, d = x.shape
    n_pool = len(POOL_WINDOWS)
    tile = pl.BlockSpec((1, ts, d), lambda i, j: (i, j, 0))
    return pl.pallas_call(
        functools.partial(_even_kernel, ts=ts),
        out_shape=jax.ShapeDtypeStruct(x.shape, x.dtype),
        grid=(b, seq // ts),
        in_specs=[tile, _full((1, d)), _full((1, d)), _full((d, EVEN_IN)),
                  pl.BlockSpec(memory_space=pltpu.SMEM),
                  _full((n_pool, POOL_GC, POOL_GC)), _full((1, POOL_WIDTH)),
                  _full((d, d))],
        out_specs=tile,
        scratch_shapes=[
            pltpu.VMEM((KV_HEADS, 2, ts + BLOCK, LANES), _BF16),
            pltpu.VMEM((KV_HEADS, 2, ts + BLOCK, LANES), _BF16),
            pltpu.VMEM((ts, ATTN_WIDTH), _BF16),
            pltpu.VMEM((ts, ATTN_WIDTH), _F32),
            pltpu.VMEM((n_pool, ts + POOL_HIST, POOL_GC), _F32),
            pltpu.VMEM((ts + POOL_HIST, POOL_GC), _F32),
            pltpu.VMEM((ts + POOL_HIST, POOL_GC), _F32),
            pltpu.VMEM((2, KV_HEADS, 2 * BLOCK, 4 * BLOCK), _F32),
            pltpu.VMEM((2, 2 * BLOCK, 4 * BLOCK), _F32),
        ],
        compiler_params=_compiler_params(),
        name="even_layer",
    )(x, gpre, gpost, w_in, sinks, pool_w, pool_scale, w_out)


def _odd_layer(x, gpre, gpost, w_in, dw_w, dw_b, ln_g, ln_b, w_out, ts):
    b, seq, d = x.shape
    tile = pl.BlockSpec((1, ts, d), lambda i, j: (i, j, 0))
    return pl.pallas_call(
        functools.partial(_odd_kernel, ts=ts),
        out_shape=jax.ShapeDtypeStruct(x.shape, x.dtype),
        grid=(b, seq // ts),
        in_specs=[tile, _full((1, d)), _full((1, d)), _full((d, ODD_IN)), _full((CONV_K, d)),
                  _full((1, d)), _full((1, d)), _full((1, d)), _full((d, d))],
        out_specs=tile,
        scratch_shapes=[
            pltpu.VMEM((N_SLAB, ts + CONV_HIST, LANES), _F32),
            pltpu.VMEM((ts, d), _F32),
        ],
        compiler_params=_compiler_params(),
        name="odd_layer",
    )(x, gpre, gpost, w_in, dw_w, dw_b, ln_g, ln_b, w_out)


def kernel(x, pre_norm, post_norm, a_w_in, a_sinks, b_pool_w, b_pool_scale, ab_w_out,
           c_w_in, c_dw_w, c_dw_b, c_ln_g, c_ln_b, c_w_out):
    depth = pre_norm.shape[0]
    ts = SEQ_TILE
    assert x.shape[1] % ts == 0 and ts % BLOCK == 0 and ts % CONV_ROWS == 0
    row = lambda v: v.reshape(1, -1).astype(_F32)
    for layer in range(depth):
        i = layer // 2
        if layer % 2 == 0:
            x = _even_layer(x, row(pre_norm[layer]), row(post_norm[layer]),
                            a_w_in[i].astype(_BF16), a_sinks[i].astype(_F32),
                            b_pool_w[i].astype(_BF16), row(b_pool_scale[i]),
                            ab_w_out[i].astype(_BF16), ts)
        else:
            x = _odd_layer(x, row(pre_norm[layer]), row(post_norm[layer]),
                           c_w_in[i].astype(_BF16),
                           c_dw_w[i].reshape(CONV_K, -1).astype(_F32),
                           row(c_dw_b[i]), row(c_ln_g[i]), row(c_ln_b[i]),
                           c_w_out[i].astype(_BF16), ts)
    return x
```

```python
import functools

import jax
import jax.numpy as jnp
from jax import lax
from jax.experimental import pallas as pl
from jax.experimental.pallas import tpu as pltpu

D_MODEL = 1024
HEAD_DIM = 64
Q_HEADS = 8
KV_HEADS = 2
ATTN_WIDTH = Q_HEADS * HEAD_DIM
KV_WIDTH = KV_HEADS * HEAD_DIM
BLOCK = 128
POOL_WIDTH = D_MODEL - ATTN_WIDTH
POOL_WINDOWS = (2, 4, 8, 16)
POOL_GC = POOL_WIDTH // len(POOL_WINDOWS)
EVEN_IN = ATTN_WIDTH + 2 * KV_WIDTH + ATTN_WIDTH + POOL_WIDTH + POOL_WIDTH
CONV_K = 31
ODD_IN = 3 * D_MODEL
EPS = 1e-6
NEG = -1e30
LOG2E = 1.4426950408889634

LANES = 128
SUBLANES = 8
N_SLAB = D_MODEL // LANES
SEQ_TILE = 1024
POOL_HIST = 24
CONV_HIST = 32
CONV_ROWS = 64
VMEM_LIMIT_BYTES = 56 * 1024 * 1024

_F32 = jnp.float32
_BF16 = jnp.bfloat16


def _rms(xf, g):
    return xf * lax.rsqrt(jnp.mean(xf * xf, axis=-1, keepdims=True) + EPS) * g


def _silu(v):
    return v * jax.nn.sigmoid(v)


def _build_attn_tables(bias_ref, mask_ref):
    shape = (2 * BLOCK, 4 * BLOCK)
    r = lax.broadcasted_iota(jnp.int32, shape, 0)
    col = lax.broadcasted_iota(jnp.int32, shape, 1)
    c = r >> 7
    i = r & (BLOCK - 1)
    e = col >> 8
    sj = col & (2 * BLOCK - 1)
    dist = i + BLOCK - sj
    valid0 = (dist >= 0) & (dist < BLOCK)
    distf = dist.astype(_F32)
    for var in range(2):
        valid = valid0 & (sj >= BLOCK) if var == 1 else valid0
        mask_ref[var] = jnp.where(valid, 1.0, 0.0).astype(_F32)
        for j in range(KV_HEADS):
            head = 4 * j + 2 * c + e
            slope = jnp.exp2(-(head + 1).astype(_F32))
            bias_ref[var, j] = jnp.where(valid, -slope * LOG2E * distf, NEG)


def _even_kernel(x_ref, gpre_ref, gpost_ref, win_ref, sink_ref, poolw_ref, pscale_ref, wout_ref,
                 o_ref, kst_ref, vst_ref, q_ref, att_ref, u_ref, t1_ref, t2_ref, bias_ref, mask_ref,
                 *, ts):
    s = pl.program_id(1)
    nblk = ts // BLOCK

    @pl.when(s == 0)
    def _start_of_sequence():
        _build_attn_tables(bias_ref, mask_ref)
        kst_ref[:, :, 0:BLOCK, :] = jnp.zeros((KV_HEADS, 2, BLOCK, LANES), _BF16)
        vst_ref[:, :, 0:BLOCK, :] = jnp.zeros((KV_HEADS, 2, BLOCK, LANES), _BF16)
        u_ref[:, 0:POOL_HIST, :] = jnp.zeros((len(POOL_WINDOWS), POOL_HIST, POOL_GC), _F32)
        t1_ref[0:SUBLANES, :] = jnp.zeros((SUBLANES, POOL_GC), _F32)
        t2_ref[0:SUBLANES, :] = jnp.zeros((SUBLANES, POOL_GC), _F32)

    @pl.when(s > 0)
    def _carry_history():
        kst_ref[:, :, 0:BLOCK, :] = kst_ref[:, :, ts:ts + BLOCK, :]
        vst_ref[:, :, 0:BLOCK, :] = vst_ref[:, :, ts:ts + BLOCK, :]
        u_ref[:, SUBLANES:POOL_HIST, :] = u_ref[:, ts + SUBLANES:ts + POOL_HIST, :]

    x = x_ref[0]
    h = _rms(x, gpre_ref[...]).astype(_BF16)
    proj = jnp.dot(h, win_ref[...], preferred_element_type=_F32)

    o_q, o_k, o_v = 0, ATTN_WIDTH, ATTN_WIDTH + KV_WIDTH
    o_ga = o_v + KV_WIDTH
    o_u = o_ga + ATTN_WIDTH
    o_gb = o_u + POOL_WIDTH

    q_ref[...] = (proj[:, o_q:o_q + ATTN_WIDTH] * (HEAD_DIM ** -0.5 * LOG2E)).astype(_BF16)

    lane = lax.broadcasted_iota(jnp.int32, (ts, LANES), 1)
    lo = lane < HEAD_DIM
    for src_off, dst in ((o_k, kst_ref), (o_v, vst_ref)):
        t2 = proj[:, src_off:src_off + KV_WIDTH]
        t2r = pltpu.roll(t2, HEAD_DIM, 1)
        dst[0, 0, BLOCK:, :] = jnp.where(lo, t2, 0.0).astype(_BF16)
        dst[0, 1, BLOCK:, :] = jnp.where(lo, 0.0, t2r).astype(_BF16)
        dst[1, 0, BLOCK:, :] = jnp.where(lo, t2r, 0.0).astype(_BF16)
        dst[1, 1, BLOCK:, :] = jnp.where(lo, 0.0, t2).astype(_BF16)

    lane_b = lax.broadcasted_iota(jnp.int32, (2 * BLOCK, LANES), 1)
    ones_even = jnp.where(lane_b < HEAD_DIM, 1.0, 0.0).astype(_BF16)
    ones_odd = jnp.where(lane_b < HEAD_DIM, 0.0, 1.0).astype(_BF16)
    lo_blk = lax.broadcasted_iota(jnp.int32, (BLOCK, LANES), 1) < HEAD_DIM

    def attn_block(n):
        r0 = n * BLOCK
        var = jnp.where(s == 0, 1, 0) if n == 0 else 0
        msk = mask_ref[var]
        for j in range(KV_HEADS):
            qcat = jnp.concatenate(
                [q_ref[pl.ds(r0, BLOCK), (2 * j) * LANES:(2 * j + 1) * LANES],
                 q_ref[pl.ds(r0, BLOCK), (2 * j + 1) * LANES:(2 * j + 2) * LANES]], axis=0)
            kcat = jnp.concatenate(
                [kst_ref[j, 0, pl.ds(r0, 2 * BLOCK), :], kst_ref[j, 1, pl.ds(r0, 2 * BLOCK), :]], axis=0)
            sc = lax.dot_general(qcat, kcat, (((1,), (1,)), ((), ())),
                                 preferred_element_type=_F32)
            sc = sc * msk + bias_ref[var, j]
            p_rows, ex_rows = [], []
            for c in range(2):
                sc_c = sc[c * BLOCK:(c + 1) * BLOCK]
                halves, exs = [], []
                for e in range(2):
                    sink = sink_ref[4 * j + 2 * c + e] * LOG2E
                    sc_h = sc_c[:, e * 2 * BLOCK:(e + 1) * 2 * BLOCK]
                    m = jnp.maximum(jnp.max(sc_h, axis=-1, keepdims=True), sink)
                    halves.append(jnp.exp2(sc_h - m))
                    exs.append(jnp.exp2(sink - m))
                p_rows.append(jnp.concatenate(halves, axis=1))
                ex_rows.append(jnp.where(lo_blk, exs[0], exs[1]))
            p = jnp.concatenate(p_rows, axis=0).astype(_BF16)
            ex = jnp.concatenate(ex_rows, axis=0)
            vcat = jnp.concatenate(
                [jnp.concatenate([vst_ref[j, 0, pl.ds(r0, 2 * BLOCK), :], ones_even], axis=1),
                 jnp.concatenate([vst_ref[j, 1, pl.ds(r0, 2 * BLOCK), :], ones_odd], axis=1)], axis=0)
            res = jnp.dot(p, vcat, preferred_element_type=_F32)
            o = res[:, 0:LANES] / (ex + res[:, LANES:2 * LANES])
            att_ref[pl.ds(r0, BLOCK), (2 * j) * LANES:(2 * j + 1) * LANES] = o[0:BLOCK]
            att_ref[pl.ds(r0, BLOCK), (2 * j + 1) * LANES:(2 * j + 2) * LANES] = o[BLOCK:2 * BLOCK]

    for n in range(nblk):
        attn_block(n)

    ya = att_ref[...] * _silu(proj[:, o_ga:o_ga + ATTN_WIDTH])

    ext = ts + POOL_HIST - SUBLANES
    t_abs = s * ts + lax.broadcasted_iota(jnp.int32, (ts, POOL_GC), 0)
    yb_parts = []
    for g, w in enumerate(POOL_WINDOWS):
        cols = slice(g * POOL_GC, (g + 1) * POOL_GC)
        u_ref[g, POOL_HIST:, :] = proj[:, o_u + g * POOL_GC:o_u + (g + 1) * POOL_GC]
        src = u_ref.at[g]
        span = 1
        bufs = (t1_ref, t2_ref)
        k = 0
        while span < w:
            cur = src[pl.ds(SUBLANES, ext), :] + src[pl.ds(SUBLANES - span, ext, stride=1), :]
            span *= 2
            if span < w:
                bufs[k % 2][pl.ds(SUBLANES, ext), :] = cur
                src = bufs[k % 2]
                k += 1
        win_sum = cur[POOL_HIST - SUBLANES:, :]
        cnt = jnp.minimum(t_abs + 1, w).astype(_F32)
        u_g = u_ref[g, pl.ds(POOL_HIST, ts), :]
        pooled = (win_sum / cnt - u_g).astype(_BF16)
        y_g = jnp.dot(pooled, poolw_ref[g], preferred_element_type=_F32)
        yb_parts.append(y_g * pscale_ref[:, cols])
    yb = jnp.concatenate(yb_parts, axis=1) * _silu(proj[:, o_gb:o_gb + POOL_WIDTH])

    y = jnp.concatenate([ya, yb], axis=1).astype(_BF16)
    mix = jnp.dot(y, wout_ref[...], preferred_element_type=_F32)
    o_ref[0] = x + _rms(mix, gpost_ref[...])


def _odd_kernel(x_ref, gpre_ref, gpost_ref, win_ref, dww_ref, dwb_ref, lng_ref, lnb_ref, wout_ref,
                o_ref, g_ref, cf_ref, *, ts):
    s = pl.program_id(1)

    @pl.when(s == 0)
    def _start_of_sequence():
        g_ref[:, 0:CONV_HIST, :] = jnp.zeros((N_SLAB, CONV_HIST, LANES), _F32)

    @pl.when(s > 0)
    def _carry_history():
        g_ref[:, 0:CONV_HIST, :] = g_ref[:, ts:ts + CONV_HIST, :]

    x = x_ref[0]
    h = _rms(x, gpre_ref[...]).astype(_BF16)

    first = CONV_HIST - (CONV_K - 1)
    pair_w = 2 * LANES
    for pair in range(N_SLAB // 2):
        a2 = jnp.dot(h, win_ref[:, pair * pair_w:(pair + 1) * pair_w], preferred_element_type=_F32)
        b2 = jnp.dot(h, win_ref[:, D_MODEL + pair * pair_w:D_MODEL + (pair + 1) * pair_w],
                     preferred_element_type=_F32)
        for half in range(2):
            hl = slice(half * LANES, (half + 1) * LANES)
            g_ref[2 * pair + half, CONV_HIST:, :] = a2[:, hl] * jax.nn.sigmoid(b2[:, hl])
        for lc in (2 * pair, 2 * pair + 1):
            cols = slice(lc * LANES, (lc + 1) * LANES)
            for rc in range(ts // CONV_ROWS):
                base = rc * CONV_ROWS
                acc = jnp.broadcast_to(dwb_ref[:, cols], (CONV_ROWS, LANES))
                for j in range(CONV_K):
                    acc = acc + dww_ref[j:j + 1, cols] * g_ref[lc, pl.ds(base + first + j, CONV_ROWS, stride=1), :]
                cf_ref[base:base + CONV_ROWS, cols] = acc

    gate = jnp.dot(h, win_ref[:, 2 * D_MODEL:3 * D_MODEL], preferred_element_type=_F32)
    cf = cf_ref[...]
    mu = jnp.mean(cf, axis=-1, keepdims=True)
    d = cf - mu
    var = jnp.mean(d * d, axis=-1, keepdims=True)
    cn = d * lax.rsqrt(var + EPS) * lng_ref[...] + lnb_ref[...]
    y = (_silu(cn) * _silu(gate)).astype(_BF16)
    mix = jnp.dot(y, wout_ref[...], preferred_element_type=_F32)
    o_ref[0] = x + _rms(mix, gpost_ref[...])


def _full(shape):
    return pl.BlockSpec(shape, lambda b, s: (0,) * len(shape))


def _compiler_params(n_inputs, weight_inputs):
    fuse = [k in weight_inputs for k in range(n_inputs)]
    return pltpu.CompilerParams(dimension_semantics=("arbitrary", "arbitrary"),
                                vmem_limit_bytes=VMEM_LIMIT_BYTES, allow_input_fusion=fuse)


def _even_layer(x, gpre, gpost, w_in, sinks, pool_w, pool_scale, w_out, ts):
    b, seq, d = x.shape
    n_pool = len(POOL_WINDOWS)
    tile = pl.BlockSpec((1, ts, d), lambda i, j: (i, j, 0))
    return pl.pallas_call(
        functools.partial(_even_kernel, ts=ts),
        out_shape=jax.ShapeDtypeStruct(x.shape, x.dtype),
        grid=(b, seq // ts),
        in_specs=[tile, _full((1, d)), _full((1, d)), _full((d, EVEN_IN)),
                  pl.BlockSpec(memory_space=pltpu.SMEM),
                  _full((n_pool, POOL_GC, POOL_GC)), _full((1, POOL_WIDTH)),
                  _full((d, d))],
        out_specs=tile,
        scratch_shapes=[
            pltpu.VMEM((KV_HEADS, 2, ts + BLOCK, LANES), _BF16),
            pltpu.VMEM((KV_HEADS, 2, ts + BLOCK, LANES), _BF16),
            pltpu.VMEM((ts, ATTN_WIDTH), _BF16),
            pltpu.VMEM((ts, ATTN_WIDTH), _F32),
            pltpu.VMEM((n_pool, ts + POOL_HIST, POOL_GC), _F32),
            pltpu.VMEM((ts + POOL_HIST, POOL_GC), _F32),
            pltpu.VMEM((ts + POOL_HIST, POOL_GC), _F32),
            pltpu.VMEM((2, KV_HEADS, 2 * BLOCK, 4 * BLOCK), _F32),
            pltpu.VMEM((2, 2 * BLOCK, 4 * BLOCK), _F32),
        ],
        compiler_params=_compiler_params(8, (3, 5, 7)),
        name="even_layer",
    )(x, gpre, gpost, w_in, sinks, pool_w, pool_scale, w_out)


def _odd_layer(x, gpre, gpost, w_in, dw_w, dw_b, ln_g, ln_b, w_out, ts):
    b, seq, d = x.shape
    tile = pl.BlockSpec((1, ts, d), lambda i, j: (i, j, 0))
    return pl.pallas_call(
        functools.partial(_odd_kernel, ts=ts),
        out_shape=jax.ShapeDtypeStruct(x.shape, x.dtype),
        grid=(b, seq // ts),
        in_specs=[tile, _full((1, d)), _full((1, d)), _full((d, ODD_IN)), _full((CONV_K, d)),
                  _full((1, d)), _full((1, d)), _full((1, d)), _full((d, d))],
        out_specs=tile,
        scratch_shapes=[
            pltpu.VMEM((N_SLAB, ts + CONV_HIST, LANES), _F32),
            pltpu.VMEM((ts, d), _F32),
        ],
        compiler_params=_compiler_params(9, (3, 8)),
        name="odd_layer",
    )(x, gpre, gpost, w_in, dw_w, dw_b, ln_g, ln_b, w_out)


def kernel(x, pre_norm, post_norm, a_w_in, a_sinks, b_pool_w, b_pool_scale, ab_w_out,
           c_w_in, c_dw_w, c_dw_b, c_ln_g, c_ln_b, c_w_out):
    depth = pre_norm.shape[0]
    ts = SEQ_TILE
    assert x.shape[1] % ts == 0 and ts % BLOCK == 0 and ts % CONV_ROWS == 0
    row = lambda v: v.reshape(1, -1).astype(_F32)
    for layer in range(depth):
        i = layer // 2
        if layer % 2 == 0:
            x = _even_layer(x, row(pre_norm[layer]), row(post_norm[layer]),
                            a_w_in[i].astype(_BF16), a_sinks[i].astype(_F32),
                            b_pool_w[i].astype(_BF16), row(b_pool_scale[i]),
                            ab_w_out[i].astype(_BF16), ts)
        else:
            x = _odd_layer(x, row(pre_norm[layer]), row(post_norm[layer]),
                           c_w_in[i].astype(_BF16),
                           c_dw_w[i].reshape(CONV_K, -1).astype(_F32),
                           row(c_dw_b[i]), row(c_ln_g[i]), row(c_ln_b[i]),
                           c_w_out[i].astype(_BF16), ts)
    return x
```
